```python
import jax, jax.numpy as jnp
from jax import lax
import numpy as np

D_MODEL = 1024
BATCH = 32
SEQ = 2048
DEPTH = 1
DEC_BATCH = 8
DEC_SEQ = 32
PAST_LEN = 1024

CHUNK = 64
N_META = 16
D_A = D_MODEL // 2
HEAD_A = 64
H_A = D_A // HEAD_A
LORA_W = 64
LORA_A = 64
LORA_G = 160
RWKV_COLS = 3 * D_A + LORA_W + LORA_A + LORA_G
LNX_EPS = 64e-5
D_B = D_MODEL - D_A
HEAD_DIM = 64
H_B = D_B // HEAD_DIM
KV_HEADS = 2
GQA = H_B // KV_HEADS
WINDOW = 128
WIN_CHUNKS = WINDOW // CHUNK
ROT_DIMS = HEAD_DIM // 4
ROPE_THETA = 500000.0
IN_COLS = RWKV_COLS + D_B + 2 * KV_HEADS * HEAD_DIM
N_EXPERTS = 32
TOP_K = 4
D_FF = D_MODEL
SWIGLU_LIMIT = 7.0
SWIGLU_ALPHA = 1.702
MOE_BLOCK = 128
DN_ALPHA = (2 * DEPTH) ** 0.25
DN_BETA = (8 * DEPTH) ** -0.25
LN_EPS = 1e-5

kernel_name = 'hymba_rwkv7_swa_moe_stream_step'


def layer_norm(x, w, b):
    xf = x.astype(jnp.float32)
    mu = jnp.mean(xf, axis=-1, keepdims=True)
    var = jnp.mean(jnp.square(xf - mu), axis=-1, keepdims=True)
    return ((xf - mu) * lax.rsqrt(var + LN_EPS) * w + b).astype(x.dtype)


def partial_rope(x, pos):
    half = ROT_DIMS // 2
    inv = ROPE_THETA ** (-2.0 * jnp.arange(half, dtype=jnp.float32) / ROT_DIMS)
    ang = pos.astype(jnp.float32)[:, None] * inv[None, :]
    cos = jnp.cos(ang)[None, :, None, :]
    sin = jnp.sin(ang)[None, :, None, :]
    x1 = x[..., :half].astype(jnp.float32)
    x2 = x[..., half:ROT_DIMS].astype(jnp.float32)
    rot = jnp.concatenate([x1 * cos - x2 * sin, x2 * cos + x1 * sin], axis=-1).astype(x.dtype)
    return jnp.concatenate([rot, x[..., ROT_DIMS:]], axis=-1)


def in_projection(h, w_in, pos):
    b, t, _ = h.shape
    z = h @ w_in
    za, q, k, v = jnp.split(z, [RWKV_COLS, RWKV_COLS + D_B, RWKV_COLS + D_B + KV_HEADS * HEAD_DIM], axis=-1)
    q = partial_rope(q.reshape(b, t, H_B, HEAD_DIM), pos)
    k = partial_rope(k.reshape(b, t, KV_HEADS, HEAD_DIM), pos)
    return za, q, k, v.reshape(b, t, KV_HEADS, HEAD_DIM)


def rwkv7_time_mix(za, shift_prev, s0, mu, w0, w2, a0, a2, g2, k_k, k_a, r_k, lnx_w, lnx_b):
    b, t, _ = za.shape
    f32 = jnp.float32
    prev = jnp.concatenate([shift_prev.astype(za.dtype), za[:, :-1]], axis=1)
    zs = za + mu * (prev - za)
    r, k, v, zw, zaa, zg = jnp.split(zs, [D_A, 2 * D_A, 3 * D_A, 3 * D_A + LORA_W, 3 * D_A + LORA_W + LORA_A], axis=-1)
    w_log = -jax.nn.softplus(-(w0 + jnp.tanh(zw) @ w2).astype(f32)) - 0.5
    decay = jnp.exp(-jnp.exp(w_log))
    a = jax.nn.sigmoid((a0 + zaa @ a2).astype(f32))
    g = (jax.nn.sigmoid(zg) @ g2).astype(f32)

    def heads(u):
        return u.astype(f32).reshape(b, t, H_A, HEAD_A)

    r, k, v, decay, a = heads(r), heads(k), heads(v), heads(decay), heads(a)
    kk = k * k_k.astype(f32).reshape(H_A, HEAD_A)
    kk = kk * lax.rsqrt(jnp.maximum(jnp.sum(kk * kk, axis=-1, keepdims=True), 1e-24))
    k = k * (1.0 + (a - 1.0) * k_a.astype(f32).reshape(H_A, HEAD_A))

    def step(s, inp):
        r_t, w_t, k_t, v_t, kk_t, a_t = inp
        s_kk = jnp.einsum('bhvk,bhk->bhv', s, kk_t)
        s = (s * w_t[:, :, None, :]
             - s_kk[..., None] * (kk_t * a_t)[:, :, None, :]
             + v_t[..., None] * k_t[:, :, None, :])
        return s, jnp.einsum('bhvk,bhk->bhv', s, r_t)

    xs = tuple(jnp.moveaxis(u, 1, 0) for u in (r, decay, k, v, kk, a))
    s_t, y = lax.scan(step, s0.astype(f32), xs)
    y = jnp.moveaxis(y, 0, 1)
    y_mu = jnp.mean(y, axis=-1, keepdims=True)
    y_var = jnp.mean(jnp.square(y - y_mu), axis=-1, keepdims=True)
    y = ((y - y_mu) * lax.rsqrt(y_var + LNX_EPS) * lnx_w.astype(f32).reshape(H_A, HEAD_A)
         + lnx_b.astype(f32).reshape(H_A, HEAD_A))
    y = y + jnp.sum(r * k * r_k.astype(f32), axis=-1, keepdims=True) * v
    out = (y.reshape(b, t, D_A) * g).astype(za.dtype)
    return out, za[:, -1:], s_t.astype(s0.dtype)


def sink_attention(q, k, v, valid, sinks):
    s = jnp.einsum('bnqhgd,bnkhd->bnhgqk', q, k).astype(jnp.float32) * (HEAD_DIM ** -0.5)
    if valid is not None:
        s = jnp.where(valid[None, :, None, None], s, -jnp.inf)
    sink = sinks.astype(jnp.float32).reshape(1, 1, KV_HEADS, GQA, 1, 1)
    m = jnp.maximum(jnp.max(s, axis=-1, keepdims=True), sink)
    p = jnp.exp(s - m)
    p = p / (jnp.sum(p, axis=-1, keepdims=True) + jnp.exp(sink - m))
    return jnp.einsum('bnhgqk,bnkhd->bnqhgd', p.astype(v.dtype), v)


def swa_prompt(q, k, v, sinks):
    b, l = q.shape[:2]
    s = l - N_META
    nc = s // CHUNK
    qm = q[:, :N_META].reshape(b, 1, N_META, KV_HEADS, GQA, HEAD_DIM)
    qr = q[:, N_META:].reshape(b, nc, CHUNK, KV_HEADS, GQA, HEAD_DIM)

    def banded(u):
        up = jnp.pad(u[:, N_META:], ((0, 0), (WIN_CHUNKS * CHUNK, 0), (0, 0), (0, 0)))
        up = up.reshape(b, nc + WIN_CHUNKS, CHUNK, KV_HEADS, HEAD_DIM)
        band = jnp.concatenate([up[:, j:j + nc] for j in range(WIN_CHUNKS + 1)], axis=2)
        meta = jnp.broadcast_to(u[:, None, :N_META], (b, nc, N_META, KV_HEADS, HEAD_DIM))
        return jnp.concatenate([meta, band], axis=2)

    kb, vb = banded(k), banded(v)
    blk = jnp.arange((WIN_CHUNKS + 1) * CHUNK) // CHUNK
    band_ok = (jnp.arange(nc)[:, None] + blk[None, :] - WIN_CHUNKS) >= 0
    valid = jnp.concatenate([jnp.ones((nc, N_META), bool), band_ok], axis=1)[:, None, :]
    out_r = sink_attention(qr, kb, vb, valid, sinks).reshape(b, s, D_B)
    out_m = sink_attention(qm, k[:, None, :N_META], v[:, None, :N_META], None, sinks).reshape(b, N_META, D_B)
    return jnp.concatenate([out_m, out_r], axis=1)


def moe_ffn(x, w_router, b_router, w_gate_up, b_gate_up, w_down, b_down):
    shp = x.shape
    xt = x.reshape(-1, D_MODEL)
    n = xt.shape[0]
    logits = (xt @ w_router + b_router).astype(jnp.float32)
    top_val, top_idx = lax.top_k(logits, TOP_K)
    gate = jax.nn.softmax(top_val, axis=-1)
    n_assign = n * TOP_K
    n_blocks = -(-(n_assign + N_EXPERTS * (MOE_BLOCK - 1)) // MOE_BLOCK)
    cap = n_blocks * MOE_BLOCK
    flat_e = top_idx.reshape(-1)
    order = jnp.argsort(flat_e)
    e_sorted = flat_e[order]
    tok_sorted = (order // TOP_K).astype(jnp.int32)
    gate_sorted = gate.reshape(-1)[order]
    counts = jnp.bincount(flat_e, length=N_EXPERTS)
    padded = (counts + MOE_BLOCK - 1) // MOE_BLOCK * MOE_BLOCK
    pad_end = jnp.cumsum(padded)
    pad_start = pad_end - padded
    grp_start = jnp.cumsum(counts) - counts
    dest = pad_start[e_sorted] + jnp.arange(n_assign) - grp_start[e_sorted]
    slot_tok = jnp.zeros((cap,), jnp.int32).at[dest].set(tok_sorted)
    slot_gate = jnp.zeros((cap,), jnp.float32).at[dest].set(gate_sorted)
    block_expert = jnp.minimum(
        jnp.searchsorted(pad_end, jnp.arange(n_blocks) * MOE_BLOCK, side='right'), N_EXPERTS - 1)

    def expert_block(args):
        tok, e = args
        h = xt[tok] @ w_gate_up[e] + b_gate_up[e]
        glu = jnp.minimum(h[:, :D_FF], SWIGLU_LIMIT)
        lin = jnp.clip(h[:, D_FF:], -SWIGLU_LIMIT, SWIGLU_LIMIT)
        hh = glu * jax.nn.sigmoid(SWIGLU_ALPHA * glu) * (lin + 1.0)
        return hh @ w_down[e] + b_down[e]

    ys = lax.map(expert_block, (slot_tok.reshape(n_blocks, MOE_BLOCK), block_expert))
    ys = ys.reshape(cap, D_MODEL) * slot_gate[:, None].astype(xt.dtype)
    out = jnp.zeros_like(xt).at[slot_tok].add(ys.astype(xt.dtype))
    return out.reshape(shp)


def post_blocks(h, ya, yb, w_out, ln1_w, ln1_b, w_router, b_router, w_gate_up, b_gate_up, w_down, b_down, ln2_w, ln2_b):
    h = layer_norm(DN_ALPHA * h + jnp.concatenate([ya, yb], axis=-1) @ w_out, ln1_w, ln1_b)
    return layer_norm(DN_ALPHA * h + moe_ffn(h, w_router, b_router, w_gate_up, b_gate_up, w_down, b_down), ln2_w, ln2_b)


def setup_inputs(seed: int = 0) -> dict:
    key = jax.random.key(seed)
    ks = iter(jax.random.split(key, 48))
    f32 = jnp.float32
    L = DEPTH

    def nrm(shape, scale):
        return jax.random.normal(next(ks), shape, f32) * scale

    def gain(shape):
        return 1.0 + nrm(shape, 0.01)

    return {
        'x_prompt': nrm((BATCH, SEQ, D_MODEL), 1.0),
        'x_sample': nrm((DEC_BATCH, DEC_SEQ, D_MODEL), 1.0),
        'state_shift': nrm((L, DEC_BATCH, 1, RWKV_COLS), 1.0),
        'state_wkv': nrm((L, DEC_BATCH, H_A, HEAD_A, HEAD_A), 0.1),
        'cache_meta_k': nrm((L, DEC_BATCH, N_META, KV_HEADS, HEAD_DIM), 1.0),
        'cache_meta_v': nrm((L, DEC_BATCH, N_META, KV_HEADS, HEAD_DIM), 1.0),
        'cache_win_k': nrm((L, DEC_BATCH, WINDOW, KV_HEADS, HEAD_DIM), 1.0),
        'cache_win_v': nrm((L, DEC_BATCH, WINDOW, KV_HEADS, HEAD_DIM), 1.0),
        'meta_tokens': nrm((N_META, D_MODEL), 1.0),
        'ln_in_w': gain((D_MODEL,)),
        'ln_in_b': nrm((D_MODEL,), 0.01),
        'w_in': nrm((L, D_MODEL, IN_COLS), D_MODEL ** -0.5),
        'mu_shift': jax.random.uniform(next(ks), (L, RWKV_COLS), f32),
        'decay_w0': -6.0 + 5.0 * jax.random.uniform(next(ks), (L, D_A), f32),
        'decay_w2': nrm((L, LORA_W, D_A), 0.5 * LORA_W ** -0.5),
        'iclr_a0': nrm((L, D_A), 0.5),
        'iclr_a2': nrm((L, LORA_A, D_A), 0.5 * LORA_A ** -0.5),
        'gate_g2': nrm((L, LORA_G, D_A), LORA_G ** -0.5),
        'k_k': 0.85 + nrm((L, D_A), 0.02),
        'k_a': gain((L, D_A)),
        'r_k': nrm((L, H_A, HEAD_A), 0.1),
        'lnx_w': gain((L, D_A)),
        'lnx_b': nrm((L, D_A), 0.01),
        'attn_sinks': nrm((L, H_B), 0.5),
        'w_out': nrm((L, D_MODEL, D_MODEL), DN_BETA * D_MODEL ** -0.5),
        'ln1_w': gain((L, D_MODEL)),
        'ln1_b': nrm((L, D_MODEL), 0.01),
        'w_router': nrm((L, D_MODEL, N_EXPERTS), D_MODEL ** -0.5),
        'b_router': nrm((L, N_EXPERTS), 0.01),
        'w_gate_up': nrm((L, N_EXPERTS, D_MODEL, 2 * D_FF), D_MODEL ** -0.5),
        'b_gate_up': nrm((L, N_EXPERTS, 2 * D_FF), 0.01),
        'w_down': nrm((L, N_EXPERTS, D_FF, D_MODEL), DN_BETA * D_FF ** -0.5),
        'b_down': nrm((L, N_EXPERTS, D_MODEL), 0.01),
        'ln2_w': gain((L, D_MODEL)),
        'ln2_b': nrm((L, D_MODEL), 0.01),
    }


def reference(x_prompt, x_sample, state_shift, state_wkv, cache_meta_k, cache_meta_v, cache_win_k, cache_win_v,
              meta_tokens, ln_in_w, ln_in_b, w_in, mu_shift, decay_w0, decay_w2, iclr_a0, iclr_a2, gate_g2,
              k_k, k_a, r_k, lnx_w, lnx_b, attn_sinks, w_out, ln1_w, ln1_b,
              w_router, b_router, w_gate_up, b_gate_up, w_down, b_down, ln2_w, ln2_b):
    b, s, _ = x_prompt.shape
    db, t, _ = x_sample.shape
    meta = jnp.broadcast_to(meta_tokens[None].astype(x_prompt.dtype), (b, N_META, D_MODEL))
    hp = layer_norm(jnp.concatenate([meta, x_prompt], axis=1), ln_in_w, ln_in_b)
    hs = layer_norm(x_sample, ln_in_w, ln_in_b)
    pos_p = jnp.arange(N_META + s)
    pos_s = N_META + PAST_LEN + jnp.arange(t)
    shift_p, wkv_p, meta_k_p, meta_v_p, win_k_p, win_v_p = [], [], [], [], [], []
    shift_s, wkv_s, win_k_s, win_v_s = [], [], [], []
    for l in range(DEPTH):
        rw = (mu_shift[l], decay_w0[l], decay_w2[l], iclr_a0[l], iclr_a2[l], gate_g2[l],
              k_k[l], k_a[l], r_k[l], lnx_w[l], lnx_b[l])
        ffn = (w_out[l], ln1_w[l], ln1_b[l], w_router[l], b_router[l], w_gate_up[l], b_gate_up[l],
               w_down[l], b_down[l], ln2_w[l], ln2_b[l])
        za, q, k, v = in_projection(hp, w_in[l], pos_p)
        ya, sh, st = rwkv7_time_mix(za, jnp.zeros((b, 1, RWKV_COLS), za.dtype),
                                    jnp.zeros((b, H_A, HEAD_A, HEAD_A), state_wkv.dtype), *rw)
        yb = swa_prompt(q, k, v, attn_sinks[l])
        hp = post_blocks(hp, ya, yb, *ffn)
        shift_p.append(sh)
        wkv_p.append(st)
        meta_k_p.append(k[:, :N_META])
        meta_v_p.append(v[:, :N_META])
        win_k_p.append(k[:, -WINDOW:])
        win_v_p.append(v[:, -WINDOW:])
        za, q, k, v = in_projection(hs, w_in[l], pos_s)
        ya, sh, st = rwkv7_time_mix(za, state_shift[l], state_wkv[l], *rw)
        keys = jnp.concatenate([cache_meta_k[l].astype(k.dtype), cache_win_k[l].astype(k.dtype), k], axis=1)
        vals = jnp.concatenate([cache_meta_v[l].astype(v.dtype), cache_win_v[l].astype(v.dtype), v], axis=1)
        yb = sink_attention(q.reshape(db, 1, t, KV_HEADS, GQA, HEAD_DIM), keys[:, None], vals[:, None],
                            None, attn_sinks[l]).reshape(db, t, D_B)
        hs = post_blocks(hs, ya, yb, *ffn)
        shift_s.append(sh)
        wkv_s.append(st)
        win_k_s.append(keys[:, -WINDOW:])
        win_v_s.append(vals[:, -WINDOW:])
    return (hp[:, N_META:], hs,
            jnp.stack(shift_p), jnp.stack(wkv_p), jnp.stack(meta_k_p), jnp.stack(meta_v_p),
            jnp.stack(win_k_p), jnp.stack(win_v_p),
            jnp.stack(shift_s), jnp.stack(wkv_s), jnp.stack(win_k_s), jnp.stack(win_v_s))
```

```python
import functools

import jax
import jax.numpy as jnp
from jax import lax
from jax.experimental import pallas as pl
from jax.experimental.pallas import tpu as pltpu

F32 = jnp.float32
BF16 = jnp.bfloat16

D_MODEL = 1024
N_META = 16
CHUNK = 64
PAST_LEN = 1024
D_A = 512
HEAD_A = 64
H_A = 8
LORA_W = 64
LORA_A = 64
LORA_G = 160
RWKV_COLS = 3 * D_A + LORA_W + LORA_A + LORA_G
LNX_EPS = 64e-5
D_B = 512
HEAD_DIM = 64
H_B = 8
KV_HEADS = 2
WINDOW = 128
ROT_HALF = 8
ROPE_THETA = 500000.0
N_EXPERTS = 32
TOP_K = 4
D_FF = 1024
SWIGLU_LIMIT = 7.0
SWIGLU_ALPHA = 1.702
DN_ALPHA = 2.0 ** 0.25
LN_EPS = 1e-5

LANES = 128
ROW_TILE = 256
ZA_PAD = 1920
Q_OFF = ZA_PAD
K_OFF = Q_OFF + D_B
V_OFF = K_OFF + KV_HEADS * HEAD_DIM
PROJ_COLS = V_OFF + KV_HEADS * HEAD_DIM
SCAN_TB = 16
MOE_BLK = 512
KEY_PAD = 64
VMEM_LIMIT = 48 * 1024 * 1024


def _cparams(sem):
    return pltpu.CompilerParams(dimension_semantics=sem, vmem_limit_bytes=VMEM_LIMIT)


def _sigmoid(x):
    return 1.0 / (1.0 + jnp.exp(-x))


def _layer_norm(x, w, b):
    mu = jnp.mean(x, axis=-1, keepdims=True)
    xc = x - mu
    var = jnp.mean(xc * xc, axis=-1, keepdims=True)
    return xc * lax.rsqrt(var + LN_EPS) * w + b


def _split3(x):
    hi = x.astype(BF16)
    r1 = x - hi.astype(F32)
    mid = r1.astype(BF16)
    lo = (r1 - mid.astype(F32)).astype(BF16)
    return hi, mid, lo


def _segsum(x, ones_bd):
    hi, mid, lo = _split3(x)
    acc = jnp.dot(hi, ones_bd, preferred_element_type=F32)
    acc = acc + jnp.dot(mid, ones_bd, preferred_element_type=F32)
    return acc + jnp.dot(lo, ones_bd, preferred_element_type=F32)


def _rows_to_matrix(ref3, lead=()):
    return jnp.concatenate([ref3[lead + (slice(None), j, slice(None))] for j in range(D_MODEL // LANES)], axis=-1)


def _ln_inproj_kernel(x_ref, lnw_ref, lnb_ref, w_ref, rc_ref, rs1_ref, rs2_ref,
                      h_ref, za_ref, q_ref, k_ref, v_ref):
    h = _layer_norm(x_ref[...], lnw_ref[...], lnb_ref[...])
    h_ref[...] = h
    z = jnp.dot(h.astype(BF16), w_ref[...], preferred_element_type=F32)
    za_ref[...] = z[:, :ZA_PAD]
    c, s1, s2 = rc_ref[0], rs1_ref[0], rs2_ref[0]

    def rope(u):
        width = u.shape[1]
        reps = width // LANES
        ct = jnp.concatenate([c] * reps, axis=1)
        s1t = jnp.concatenate([s1] * reps, axis=1)
        s2t = jnp.concatenate([s2] * reps, axis=1)
        return u * ct + pltpu.roll(u, width - ROT_HALF, 1) * s1t + pltpu.roll(u, ROT_HALF, 1) * s2t

    q_ref[...] = rope(z[:, Q_OFF:K_OFF])
    k_ref[...] = rope(z[:, K_OFF:V_OFF])
    v_ref[...] = z[:, V_OFF:PROJ_COLS]


def _ln_inproj(x_all, lnw, lnb, w_all, rope_c, rope_s1, rope_s2, n_prompt_tiles, tiles_per_seq):
    n = x_all.shape[0]
    nt = n // ROW_TILE

    def tbl_map(i):
        return (jnp.where(i < n_prompt_tiles, i % tiles_per_seq, tiles_per_seq + i - n_prompt_tiles), 0, 0)

    row = lambda i: (i, 0)
    const = lambda i: (0, 0)
    tbl = pl.BlockSpec((1, ROW_TILE, LANES), tbl_map)
    return pl.pallas_call(
        _ln_inproj_kernel,
        grid=(nt,),
        in_specs=[pl.BlockSpec((ROW_TILE, D_MODEL), row),
                  pl.BlockSpec((1, D_MODEL), const), pl.BlockSpec((1, D_MODEL), const),
                  pl.BlockSpec((D_MODEL, PROJ_COLS), const), tbl, tbl, tbl],
        out_specs=[pl.BlockSpec((ROW_TILE, D_MODEL), row), pl.BlockSpec((ROW_TILE, ZA_PAD), row),
                   pl.BlockSpec((ROW_TILE, D_B), row), pl.BlockSpec((ROW_TILE, LANES), row),
                   pl.BlockSpec((ROW_TILE, LANES), row)],
        out_shape=[jax.ShapeDtypeStruct((n, D_MODEL), F32), jax.ShapeDtypeStruct((n, ZA_PAD), F32),
                   jax.ShapeDtypeStruct((n, D_B), F32), jax.ShapeDtypeStruct((n, LANES), F32),
                   jax.ShapeDtypeStruct((n, LANES), F32)],
        compiler_params=_cparams(("arbitrary",)),
        name="ln_inproj",
    )(x_all, lnw, lnb, w_all, rope_c, rope_s1, rope_s2)


def _rwkv_pre_kernel(flags_ref, za_ref, p0_ref, mu_ref, w0_ref, w2_ref, a0_ref, a2_ref, g2_ref,
                     kk_w_ref, ka_ref, rk_ref, ones_ref,
                     r_o, w_o, k_o, v_o, kk_o, b_o, cv_o, g_o):
    i = pl.program_id(0)
    z = za_ref[...]
    prev = pltpu.roll(z, 1, 0)
    row = lax.broadcasted_iota(jnp.int32, (ROW_TILE, 1), 0)
    bits = flags_ref[i]
    for m in range(8):
        use = ((bits >> m) & 1) == 1
        sel = jnp.logical_and(row == 32 * m, use)
        prev = jnp.where(sel, p0_ref[0, m:m + 1, :], prev)
    zs = z + mu_ref[...] * (prev - z)
    r = zs[:, 0:D_A]
    k = zs[:, D_A:2 * D_A]
    v = zs[:, 2 * D_A:3 * D_A]
    l_wa = zs[:, 3 * D_A:3 * D_A + LANES]
    l_g = zs[:, 3 * D_A + LANES:ZA_PAD]
    wl = w0_ref[...] + jnp.dot(jnp.tanh(l_wa).astype(BF16), w2_ref[...], preferred_element_type=F32)
    nwl = -wl
    softplus = jnp.maximum(nwl, 0.0) + jnp.log(1.0 + jnp.exp(-jnp.abs(nwl)))
    w_log = -softplus - 0.5
    decay = jnp.exp(-jnp.exp(w_log))
    a = _sigmoid(a0_ref[...] + jnp.dot(l_wa.astype(BF16), a2_ref[...], preferred_element_type=F32))
    g = jnp.dot(_sigmoid(l_g).astype(BF16), g2_ref[...], preferred_element_type=F32)
    ones_bd = ones_ref[...]
    kk = k * kk_w_ref[...]
    kk = kk * lax.rsqrt(jnp.maximum(_segsum(kk * kk, ones_bd), 1e-24))
    kh = k * (1.0 + (a - 1.0) * ka_ref[...])
    bonus = _segsum(r * kh * rk_ref[...], ones_bd)
    r_o[...] = r
    w_o[...] = decay
    k_o[...] = kh
    v_o[...] = v
    kk_o[...] = kk
    b_o[...] = kk * a
    cv_o[...] = bonus * v
    g_o[...] = g


def _rwkv_pre(flags, za, p0, mu, w0, w2p, a0, a2p, g2p, kk_w, ka, rk, ones_bd):
    n = za.shape[0]
    nt = n // ROW_TILE
    row = lambda i, f: (i, 0)
    const = lambda i, f: (0, 0)
    vec = pl.BlockSpec((1, D_A), const)
    out = pl.BlockSpec((ROW_TILE, D_A), row)
    return pl.pallas_call(
        _rwkv_pre_kernel,
        grid_spec=pltpu.PrefetchScalarGridSpec(
            num_scalar_prefetch=1,
            grid=(nt,),
            in_specs=[pl.BlockSpec((ROW_TILE, ZA_PAD), row),
                      pl.BlockSpec((1, 8, ZA_PAD), lambda i, f: (i, 0, 0)),
                      pl.BlockSpec((1, ZA_PAD), const),
                      vec, pl.BlockSpec((LANES, D_A), const),
                      vec, pl.BlockSpec((LANES, D_A), const),
                      pl.BlockSpec((2 * LANES, D_A), const),
                      vec, vec, vec, pl.BlockSpec((D_A, D_A), const)],
            out_specs=[out] * 8),
        out_shape=[jax.ShapeDtypeStruct((n, D_A), F32)] * 8,
        compiler_params=_cparams(("arbitrary",)),
        name="rwkv_pre",
    )(flags, za, p0, mu, w0, w2p, a0, a2p, g2p, kk_w, ka, rk, ones_bd)


def _scan_kernel(r_ref, w_ref, k_ref, v_ref, kk_ref, b_ref, s0_ref, y_ref, st_ref, s_scr, *, n_tblocks):
    tb = pl.program_id(1)

    @pl.when(tb == 0)
    def _():
        s_scr[...] = s0_ref[0]

    def step(tt, carry):
        skk = jnp.zeros((HEAD_A, LANES), F32)
        for kx in range(HEAD_A):
            skk = skk + s_scr[kx] * kk_ref[tt, 0, pl.ds(kx, 1), :]
        vt = v_ref[tt, 0]
        y = jnp.zeros((HEAD_A, LANES), F32)
        for kx in range(HEAD_A):
            sn = (s_scr[kx] * w_ref[tt, 0, pl.ds(kx, 1), :]
                  + (vt * k_ref[tt, 0, pl.ds(kx, 1), :] - skk * b_ref[tt, 0, pl.ds(kx, 1), :]))
            s_scr[kx] = sn
            y = y + sn * r_ref[tt, 0, pl.ds(kx, 1), :]
        y_ref[tt, 0] = y
        return carry

    lax.fori_loop(0, SCAN_TB, step, 0)

    @pl.when(tb == n_tblocks - 1)
    def _():
        st_ref[0] = s_scr[...]


def _scan(ops, s0):
    t, g = ops[0].shape[:2]
    nb = t // SCAN_TB
    op_spec = pl.BlockSpec((SCAN_TB, 1, HEAD_A, LANES), lambda gi, ti: (ti, gi, 0, 0))
    st_spec = pl.BlockSpec((1, HEAD_A, HEAD_A, LANES), lambda gi, ti: (gi, 0, 0, 0))
    return pl.pallas_call(
        functools.partial(_scan_kernel, n_tblocks=nb),
        grid=(g, nb),
        in_specs=[op_spec] * 6 + [st_spec],
        out_specs=[op_spec, st_spec],
        out_shape=[jax.ShapeDtypeStruct((t, g, HEAD_A, LANES), F32),
                   jax.ShapeDtypeStruct((g, HEAD_A, HEAD_A, LANES), F32)],
        scratch_shapes=[pltpu.VMEM((HEAD_A, HEAD_A, LANES), F32)],
        compiler_params=_cparams(("arbitrary", "arbitrary")),
        name="rwkv_scan",
    )(*ops, s0)


def _chain_layout(batch):
    heads_per_group = min(H_A, LANES // batch)
    return H_A // heads_per_group, heads_per_group


def _to_chains(x, batch):
    t = x.shape[1]
    g, hg = _chain_layout(batch)
    y = x.reshape(batch, t, g, hg, HEAD_A).transpose(1, 2, 4, 3, 0).reshape(t, g, HEAD_A, hg * batch)
    return jnp.pad(y, ((0, 0), (0, 0), (0, 0), (0, LANES - hg * batch)))


def _from_chains(y, batch):
    t = y.shape[0]
    g, hg = _chain_layout(batch)
    y = y[..., :hg * batch].reshape(t, g, HEAD_A, hg, batch).transpose(4, 0, 1, 3, 2)
    return y.reshape(batch * t, D_A)


def _state_to_chains(s, batch):
    g, hg = _chain_layout(batch)
    y = s.reshape(batch, g, hg, HEAD_A, HEAD_A).transpose(1, 4, 3, 2, 0).reshape(g, HEAD_A, HEAD_A, hg * batch)
    return jnp.pad(y, ((0, 0), (0, 0), (0, 0), (0, LANES - hg * batch)))


def _state_from_chains(s, batch):
    g, hg = _chain_layout(batch)
    y = s[..., :hg * batch].reshape(g, HEAD_A, HEAD_A, hg, batch).transpose(4, 0, 3, 2, 1)
    return y.reshape(batch, H_A, HEAD_A, HEAD_A)


def _attn_kernel(q_ref, kc_ref, kh_ref, vc_ref, vh_ref, km_ref, vm_ref, sink_ref, o_ref, *, tq, qt0, valid_len):
    qi = pl.program_id(1) + qt0
    k_ext = jnp.concatenate([kh_ref[...], kc_ref[...]], axis=0)
    v_ext = jnp.concatenate([vh_ref[...], vc_ref[...]], axis=0)
    lane = lax.broadcasted_iota(jnp.int32, (1, LANES), 1)
    low = lane < HEAD_DIM
    col = lax.broadcasted_iota(jnp.int32, (1, 4 * CHUNK), 1)
    nkeys = 4 * CHUNK

    def block_diag(x256, kv):
        sw = pltpu.roll(x256, HEAD_DIM, 1)
        top, bot = (x256, sw) if kv == 0 else (sw, x256)
        top = jnp.where(low, top, 0.0)
        bot = jnp.where(low, 0.0, bot)
        return jnp.concatenate([top, bot], axis=0).astype(BF16)

    for ci in range(tq // CHUNK):
        kband = jnp.concatenate([km_ref[0], k_ext[CHUNK * ci:CHUNK * ci + 3 * CHUNK]], axis=0)
        vband = jnp.concatenate([vm_ref[0], v_ext[CHUNK * ci:CHUNK * ci + 3 * CHUNK]], axis=0)
        key_row = qi * tq + CHUNK * ci - 2 * CHUNK + (col - KEY_PAD)
        in_band = jnp.where(key_row >= 0, jnp.where(key_row < valid_len, 0.0, -jnp.inf), -jnp.inf)
        bias = jnp.where(col < N_META, 0.0, jnp.where(col >= KEY_PAD, in_band, -jnp.inf))
        for kv in range(KV_HEADS):
            k_bd = block_diag(kband, kv)
            v_bd = block_diag(vband, kv)
            for pr in range(2):
                c0 = kv * 2 * LANES + pr * LANES
                q2 = q_ref[CHUNK * ci:CHUNK * (ci + 1), c0:c0 + LANES].astype(BF16)
                s = lax.dot_general(q2, k_bd, (((1,), (1,)), ((), ())), preferred_element_type=F32)
                s = s * (HEAD_DIM ** -0.5)
                probs = []
                for sg in range(2):
                    ss = s[:, nkeys * sg:nkeys * (sg + 1)] + bias
                    sk = sink_ref[kv * 2 + pr, :, nkeys * sg:nkeys * sg + 1]
                    m = jnp.maximum(jnp.max(ss, axis=-1, keepdims=True), sk)
                    p = jnp.exp(ss - m)
                    den = jnp.sum(p, axis=-1, keepdims=True) + jnp.exp(sk - m)
                    probs.append(p / den)
                pm = jnp.concatenate(probs, axis=1).astype(BF16)
                o_ref[CHUNK * ci:CHUNK * (ci + 1), c0:c0 + LANES] = jnp.dot(pm, v_bd, preferred_element_type=F32)


def _attention(q, k, v, km, vm, sink_rows, n_seq, seq_len, tq, qt0, n_qt, valid_len):
    tiles_per_seq = seq_len // tq
    halo_per_tile = tq // WINDOW
    per_seq_meta = km.shape[0] > 1

    def cur(b, i):
        return (b * tiles_per_seq + qt0 + i, 0)

    def halo(b, i):
        return (jnp.maximum((b * tiles_per_seq + qt0 + i) * halo_per_tile - 1, 0), 0)

    def meta(b, i):
        return (b if per_seq_meta else 0, 0, 0)

    return pl.pallas_call(
        functools.partial(_attn_kernel, tq=tq, qt0=qt0, valid_len=valid_len),
        grid=(n_seq, n_qt),
        in_specs=[pl.BlockSpec((tq, D_B), cur),
                  pl.BlockSpec((tq, LANES), cur), pl.BlockSpec((WINDOW, LANES), halo),
                  pl.BlockSpec((tq, LANES), cur), pl.BlockSpec((WINDOW, LANES), halo),
                  pl.BlockSpec((1, KEY_PAD, LANES), meta), pl.BlockSpec((1, KEY_PAD, LANES), meta),
                  pl.BlockSpec((4, 1, 8 * CHUNK), lambda b, i: (0, 0, 0))],
        out_specs=pl.BlockSpec((tq, D_B), lambda b, i: (b * n_qt + i, 0)),
        out_shape=jax.ShapeDtypeStruct((n_seq * n_qt * tq, D_B), F32),
        compiler_params=_cparams(("arbitrary", "arbitrary")),
        name="sink_attention",
    )(q, k, k, v, v, km, vm, sink_rows)


def _post_mix_kernel(yp_ref, ys_ref, cv_ref, g_ref, ybp_ref, ybs_ref, h_ref, lnxw_ref, lnxb_ref, ones_ref,
                     wout_ref, ln1w_ref, ln1b_ref, wr_ref, br_ref, h1_ref, lg_ref, *, n_prompt_tiles):
    is_prompt = pl.program_id(0) < n_prompt_tiles
    y = jnp.where(is_prompt, yp_ref[...], ys_ref[...])
    yb = jnp.where(is_prompt, ybp_ref[...], ybs_ref[...])
    ones_bd = ones_ref[...]
    mean = _segsum(y, ones_bd) * (1.0 / HEAD_A)
    d = y - mean
    var = _segsum(d * d, ones_bd) * (1.0 / HEAD_A)
    yn = d * lax.rsqrt(var + LNX_EPS) * lnxw_ref[...] + lnxb_ref[...]
    ya = (yn + cv_ref[...]) * g_ref[...]
    mix = jnp.concatenate([ya, yb], axis=-1).astype(BF16)
    res = DN_ALPHA * h_ref[...] + jnp.dot(mix, wout_ref[...], preferred_element_type=F32)
    h1 = _layer_norm(res, ln1w_ref[...], ln1b_ref[...])
    for j in range(D_MODEL // LANES):
        h1_ref[:, j, :] = h1[:, LANES * j:LANES * (j + 1)]
    lg_ref[...] = jnp.dot(h1, wr_ref[...], preferred_element_type=F32,
                          precision=lax.Precision.HIGHEST) + br_ref[...]


def _post_mix(y_p, y_s, cv, g, yb_p, yb_s, h_all, lnxw, lnxb, ones_bd, wout, ln1w, ln1b, wr, br, n_prompt_tiles):
    nt = n_prompt_tiles + 1
    n1 = nt * ROW_TILE
    row = lambda i: (i, 0)
    prow = lambda i: (jnp.minimum(i, n_prompt_tiles - 1), 0)
    const = lambda i: (0, 0)
    half = lambda m: pl.BlockSpec((ROW_TILE, D_A), m)
    return pl.pallas_call(
        functools.partial(_post_mix_kernel, n_prompt_tiles=n_prompt_tiles),
        grid=(nt,),
        in_specs=[half(prow), half(const), half(row), half(row), half(prow), half(const),
                  pl.BlockSpec((ROW_TILE, D_MODEL), row),
                  pl.BlockSpec((1, D_A), const), pl.BlockSpec((1, D_A), const), pl.BlockSpec((D_A, D_A), const),
                  pl.BlockSpec((D_MODEL, D_MODEL), const),
                  pl.BlockSpec((1, D_MODEL), const), pl.BlockSpec((1, D_MODEL), const),
                  pl.BlockSpec((D_MODEL, LANES), const), pl.BlockSpec((1, LANES), const)],
        out_specs=[pl.BlockSpec((ROW_TILE, D_MODEL // LANES, LANES), lambda i: (i, 0, 0)),
                   pl.BlockSpec((ROW_TILE, LANES), row)],
        out_shape=[jax.ShapeDtypeStruct((n1, D_MODEL // LANES, LANES), F32),
                   jax.ShapeDtypeStruct((n1, LANES), F32)],
        compiler_params=_cparams(("arbitrary",)),
        name="post_mix",
    )(y_p, y_s, cv, g, yb_p, yb_s, h_all, lnxw, lnxb, ones_bd, wout, ln1w, ln1b, wr, br)


def _gather_rows(idx_ref, n_rows, src_hbm, dst, sem):
    def body(r, carry):
        pltpu.make_async_copy(src_hbm.at[idx_ref[0, 0, r]], dst.at[r], sem).start()
        return carry
    lax.fori_loop(0, n_rows, body, 0)


def _wait_rows(n_rows, src_hbm, dst, sem):
    pltpu.make_async_copy(src_hbm.at[pl.ds(0, n_rows)], dst, sem).wait()


def _expert_kernel(be_ref, tok_ref, tok_next_ref, h1_hbm, gate_ref, wgu_ref, bgu_ref, wd_ref, bd_ref,
                   ys_ref, xbuf, sems, *, n_blocks):
    i = pl.program_id(0)
    slot = i % 2

    @pl.when(i == 0)
    def _():
        _gather_rows(tok_ref, MOE_BLK, h1_hbm, xbuf.at[0], sems.at[0])

    @pl.when(i + 1 < n_blocks)
    def _():
        _gather_rows(tok_next_ref, MOE_BLK, h1_hbm, xbuf.at[1 - slot], sems.at[1 - slot])

    _wait_rows(MOE_BLK, h1_hbm, xbuf.at[slot], sems.at[slot])
    x = _rows_to_matrix(xbuf, (slot,)).astype(BF16)
    h = jnp.dot(x, wgu_ref[0], preferred_element_type=F32) + bgu_ref[0]
    glu = jnp.minimum(h[:, :D_FF], SWIGLU_LIMIT)
    lin = jnp.clip(h[:, D_FF:], -SWIGLU_LIMIT, SWIGLU_LIMIT)
    hh = glu * _sigmoid(SWIGLU_ALPHA * glu) * (lin + 1.0)
    y = jnp.dot(hh.astype(BF16), wd_ref[0], preferred_element_type=F32) + bd_ref[0]
    gate = gate_ref[...]
    for j in range(D_MODEL // LANES):
        ys_ref[:, j, :] = y[:, LANES * j:LANES * (j + 1)] * gate


def _expert_mlp(block_expert, slot_tok3, h1_rows, slot_gate, wgu, bgu, wd, bd):
    n_blocks = block_expert.shape[0]
    cap = n_blocks * MOE_BLK
    tok_spec = lambda m: pl.BlockSpec((1, 1, MOE_BLK), m, memory_space=pltpu.SMEM)
    return pl.pallas_call(
        functools.partial(_expert_kernel, n_blocks=n_blocks),
        grid_spec=pltpu.PrefetchScalarGridSpec(
            num_scalar_prefetch=1,
            grid=(n_blocks,),
            in_specs=[tok_spec(lambda i, be: (i, 0, 0)),
                      tok_spec(lambda i, be: (jnp.minimum(i + 1, n_blocks - 1), 0, 0)),
                      pl.BlockSpec(memory_space=pl.ANY),
                      pl.BlockSpec((MOE_BLK, LANES), lambda i, be: (i, 0)),
                      pl.BlockSpec((1, D_MODEL, 2 * D_FF), lambda i, be: (be[i], 0, 0)),
                      pl.BlockSpec((1, 1, 2 * D_FF), lambda i, be: (be[i], 0, 0)),
                      pl.BlockSpec((1, D_FF, D_MODEL), lambda i, be: (be[i], 0, 0)),
                      pl.BlockSpec((1, 1, D_MODEL), lambda i, be: (be[i], 0, 0))],
            out_specs=pl.BlockSpec((MOE_BLK, D_MODEL // LANES, LANES), lambda i, be: (i, 0, 0)),
            scratch_shapes=[pltpu.VMEM((2, MOE_BLK, D_MODEL // LANES, LANES), F32),
                            pltpu.SemaphoreType.DMA((2,))]),
        out_shape=jax.ShapeDtypeStruct((cap, D_MODEL // LANES, LANES), F32),
        compiler_params=_cparams(("arbitrary",)),
        name="expert_mlp",
    )(block_expert, slot_tok3, slot_tok3, h1_rows, slot_gate, wgu, bgu, wd, bd)


def _combine_kernel(pos_ref, pos_next_ref, ys_hbm, h1_ref, ln2w_ref, ln2b_ref, yp_ref, ysm_ref, gbuf, sems,
                    *, n_tiles, n_prompt_tiles):
    i = pl.program_id(0)
    slot = i % 2
    n_rows = TOP_K * ROW_TILE

    @pl.when(i == 0)
    def _():
        _gather_rows(pos_ref, n_rows, ys_hbm, gbuf.at[0], sems.at[0])

    @pl.when(i + 1 < n_tiles)
    def _():
        _gather_rows(pos_next_ref, n_rows, ys_hbm, gbuf.at[1 - slot], sems.at[1 - slot])

    _wait_rows(n_rows, ys_hbm, gbuf.at[slot], sems.at[slot])
    cols = []
    for j in range(D_MODEL // LANES):
        acc = gbuf[slot, pl.ds(0, ROW_TILE), j, :]
        for t in range(1, TOP_K):
            acc = acc + gbuf[slot, pl.ds(t * ROW_TILE, ROW_TILE), j, :]
        cols.append(acc)
    moe = jnp.concatenate(cols, axis=-1)
    out = _layer_norm(DN_ALPHA * _rows_to_matrix(h1_ref) + moe, ln2w_ref[...], ln2b_ref[...])

    @pl.when(i < n_prompt_tiles)
    def _():
        yp_ref[...] = out

    @pl.when(i >= n_prompt_tiles)
    def _():
        ysm_ref[...] = out


def _combine(pos3, ys_rows, h1_rows, ln2w, ln2b, n_prompt_tiles):
    n_tiles = n_prompt_tiles + 1
    n_rows = TOP_K * ROW_TILE
    pos_spec = lambda m: pl.BlockSpec((1, 1, n_rows), m, memory_space=pltpu.SMEM)
    const = lambda i: (0, 0)
    return pl.pallas_call(
        functools.partial(_combine_kernel, n_tiles=n_tiles, n_prompt_tiles=n_prompt_tiles),
        grid=(n_tiles,),
        in_specs=[pos_spec(lambda i: (i, 0, 0)),
                  pos_spec(lambda i: (jnp.minimum(i + 1, n_tiles - 1), 0, 0)),
                  pl.BlockSpec(memory_space=pl.ANY),
                  pl.BlockSpec((ROW_TILE, D_MODEL // LANES, LANES), lambda i: (i, 0, 0)),
                  pl.BlockSpec((1, D_MODEL), const), pl.BlockSpec((1, D_MODEL), const)],
        out_specs=[pl.BlockSpec((ROW_TILE, D_MODEL), lambda i: (jnp.minimum(i, n_prompt_tiles - 1), 0)),
                   pl.BlockSpec((ROW_TILE, D_MODEL), const)],
        out_shape=[jax.ShapeDtypeStruct((n_prompt_tiles * ROW_TILE, D_MODEL), F32),
                   jax.ShapeDtypeStruct((ROW_TILE, D_MODEL), F32)],
        scratch_shapes=[pltpu.VMEM((2, n_rows, D_MODEL // LANES, LANES), F32),
                        pltpu.SemaphoreType.DMA((2,))],
        compiler_params=_cparams(("arbitrary",)),
        name="moe_combine",
    )(pos3, pos3, ys_rows, h1_rows, ln2w, ln2b)


def _routing(logits):
    n = logits.shape[0]
    top_val, top_idx = lax.top_k(logits, TOP_K)
    gate = jax.nn.softmax(top_val, axis=-1)
    n_assign = n * TOP_K
    n_blocks = -(-(n_assign + N_EXPERTS * (MOE_BLK - 1)) // MOE_BLK)
    cap = n_blocks * MOE_BLK
    flat_e = top_idx.reshape(-1)
    order = jnp.argsort(flat_e)
    e_sorted = flat_e[order]
    counts = jnp.bincount(flat_e, length=N_EXPERTS)
    padded = (counts + MOE_BLK - 1) // MOE_BLK * MOE_BLK
    pad_end = jnp.cumsum(padded)
    pad_start = pad_end - padded
    grp_start = jnp.cumsum(counts) - counts
    dest = (pad_start[e_sorted] + jnp.arange(n_assign) - grp_start[e_sorted]).astype(jnp.int32)
    slot_tok = jnp.zeros((cap,), jnp.int32).at[dest].set((order // TOP_K).astype(jnp.int32))
    slot_gate = jnp.zeros((cap,), F32).at[dest].set(gate.reshape(-1)[order])
    block_expert = jnp.minimum(
        jnp.searchsorted(pad_end, jnp.arange(n_blocks) * MOE_BLK, side='right'), N_EXPERTS - 1).astype(jnp.int32)
    pos = jnp.zeros((n_assign,), jnp.int32).at[order].set(dest)
    return block_expert, slot_tok, slot_gate, pos


def _rope_tables(pos):
    inv = ROPE_THETA ** (-2.0 * jnp.arange(ROT_HALF, dtype=F32) / (2 * ROT_HALF))
    ang = pos.astype(F32)[:, None] * inv[None, :]
    cos, sin = jnp.cos(ang), jnp.sin(ang)
    n = pos.shape[0]
    one = jnp.ones((n, HEAD_DIM - 2 * ROT_HALF), F32)
    zero = jnp.zeros((n, HEAD_DIM - 2 * ROT_HALF), F32)
    z8 = jnp.zeros((n, ROT_HALF), F32)
    c = jnp.concatenate([cos, cos, one], axis=1)
    s1 = jnp.concatenate([-sin, z8, zero], axis=1)
    s2 = jnp.concatenate([z8, sin, zero], axis=1)
    tile2 = lambda u: jnp.concatenate([u, u], axis=1).reshape(n // ROW_TILE, ROW_TILE, LANES)
    return tile2(c), tile2(s1), tile2(s2)


def kernel(x_prompt, x_sample, state_shift, state_wkv, cache_meta_k, cache_meta_v, cache_win_k, cache_win_v, meta_tokens, ln_in_w, ln_in_b, w_in, mu_shift, decay_w0, decay_w2, iclr_a0, iclr_a2, gate_g2, k_k, k_a, r_k, lnx_w, lnx_b, attn_sinks, w_out, ln1_w, ln1_b, w_router, b_router, w_gate_up, b_gate_up, w_down, b_down, ln2_w, ln2_b):
    b, s, _ = x_prompt.shape
    db, t, _ = x_sample.shape
    assert db == 8 and t == 32 and s % ROW_TILE == 0 and w_in.shape[0] == 1
    tiles_per_seq = s // ROW_TILE
    npt = b * tiles_per_seq
    n_p = b * s
    row2 = lambda u: u.reshape(1, -1)

    meta_pad = jnp.zeros((ROW_TILE, D_MODEL), F32).at[:N_META].set(meta_tokens)
    x_all = jnp.concatenate([x_prompt.reshape(n_p, D_MODEL), x_sample.reshape(db * t, D_MODEL), meta_pad], axis=0)
    w0 = w_in[0]
    w_all = jnp.concatenate([w0[:, :RWKV_COLS], jnp.zeros((D_MODEL, ZA_PAD - RWKV_COLS), F32), w0[:, RWKV_COLS:]],
                            axis=1).astype(BF16)
    pos = jnp.concatenate([N_META + jnp.arange(s), N_META + PAST_LEN + jnp.tile(jnp.arange(t), db),
                           jnp.arange(ROW_TILE)])
    rope_c, rope_s1, rope_s2 = _rope_tables(pos)
    h_all, za, q_all, k_all, v_all = _ln_inproj(x_all, row2(ln_in_w), row2(ln_in_b), w_all,
                                                rope_c, rope_s1, rope_s2, npt, tiles_per_seq)

    meta_row0 = (npt + 1) * ROW_TILE
    za_meta_last = za[meta_row0 + N_META - 1]
    tile_last = za[ROW_TILE - 1:n_p:ROW_TILE]
    prev_first = jnp.concatenate([za_meta_last[None], tile_last[:-1]], axis=0)
    prev_first = jnp.where((jnp.arange(npt) % tiles_per_seq == 0)[:, None], za_meta_last[None], prev_first)
    p0 = jnp.zeros((npt + 2, 8, ZA_PAD), F32)
    p0 = p0.at[:npt, 0].set(prev_first)
    p0 = p0.at[npt, :, :RWKV_COLS].set(state_shift[0, :, 0])
    flags = jnp.concatenate([jnp.ones((npt,), jnp.int32), jnp.array([255, 1], jnp.int32)])
    pad_cols = lambda u, n: jnp.pad(u, ((0, 0), (0, n - u.shape[1])))
    mu = pad_cols(mu_shift, ZA_PAD)
    w2p = jnp.zeros((LANES, D_A), F32).at[:LORA_W].set(decay_w2[0]).astype(BF16)
    a2p = jnp.zeros((LANES, D_A), F32).at[LORA_W:LORA_W + LORA_A].set(iclr_a2[0]).astype(BF16)
    g2p = jnp.zeros((2 * LANES, D_A), F32).at[:LORA_G].set(gate_g2[0]).astype(BF16)
    seg = jnp.arange(D_A) // HEAD_A
    ones_bd = (seg[:, None] == seg[None, :]).astype(BF16)
    r_, w_, k_, v_, kk_, b_, cv, g = _rwkv_pre(flags, za, p0, mu, decay_w0, w2p, iclr_a0, a2p, g2p,
                                               k_k, k_a, row2(r_k[0]), ones_bd)

    def prompt_ops(u):
        um = jnp.broadcast_to(u[meta_row0:meta_row0 + N_META][None], (b, N_META, D_A))
        return _to_chains(jnp.concatenate([um, u[:n_p].reshape(b, s, D_A)], axis=1), b)

    def sample_ops(u):
        return _to_chains(u[n_p:n_p + db * t].reshape(db, t, D_A), db)

    scan_in = (r_, w_, k_, v_, kk_, b_)
    gp = _chain_layout(b)[0]
    y_pc, st_pc = _scan([prompt_ops(u) for u in scan_in], jnp.zeros((gp, HEAD_A, HEAD_A, LANES), F32))
    y_sc, st_sc = _scan([sample_ops(u) for u in scan_in], _state_to_chains(state_wkv[0], db))
    y_p = _from_chains(y_pc[N_META:], b)
    y_s = _from_chains(y_sc, db)
    wkv_p = _state_from_chains(st_pc, b)
    wkv_s = _state_from_chains(st_sc, db)

    sinks = attn_sinks[0]
    sink_rows = jnp.repeat(sinks.reshape(4, 2), 4 * CHUNK, axis=1).reshape(4, 1, 8 * CHUNK)
    kvd = KV_HEADS * HEAD_DIM
    pad_meta = lambda u: jnp.pad(u, ((0, 0), (0, KEY_PAD - N_META), (0, 0)))
    km_p = pad_meta(k_all[meta_row0:meta_row0 + N_META][None])
    vm_p = pad_meta(v_all[meta_row0:meta_row0 + N_META][None])
    tq = ROW_TILE
    yb_p = _attention(q_all, k_all, v_all, km_p, vm_p, sink_rows, b, s, tq, 0, s // tq, s)
    seq_s = 2 * WINDOW
    k_new = k_all[n_p:n_p + db * t].reshape(db, t, kvd)
    v_new = v_all[n_p:n_p + db * t].reshape(db, t, kvd)
    q_new = q_all[n_p:n_p + db * t].reshape(db, t, D_B)
    keys_s = jnp.concatenate([cache_win_k[0].reshape(db, WINDOW, kvd), k_new], axis=1)
    vals_s = jnp.concatenate([cache_win_v[0].reshape(db, WINDOW, kvd), v_new], axis=1)
    pad_seq = lambda u, front: jnp.pad(u, ((0, 0), (front, seq_s - front - u.shape[1]), (0, 0))).reshape(db * seq_s, -1)
    yb_s = _attention(pad_seq(q_new, WINDOW), pad_seq(keys_s, 0), pad_seq(vals_s, 0),
                      pad_meta(cache_meta_k[0].reshape(db, N_META, kvd)),
                      pad_meta(cache_meta_v[0].reshape(db, N_META, kvd)),
                      sink_rows, db, seq_s, WINDOW, 1, 1, WINDOW + t)
    yb_s = yb_s.reshape(db, WINDOW, D_B)[:, :t].reshape(db * t, D_B)

    wr = pad_cols(w_router[0], LANES)
    br = pad_cols(b_router, LANES)
    h1_rows, logits = _post_mix(y_p, y_s, cv, g, yb_p, yb_s, h_all, lnx_w, lnx_b, ones_bd,
                                w_out[0].astype(BF16), ln1_w, ln1_b, wr, br, npt)

    n1 = (npt + 1) * ROW_TILE
    block_expert, slot_tok, slot_gate, pos_slots = _routing(logits[:, :N_EXPERTS])
    n_blocks = block_expert.shape[0]
    ys_rows = _expert_mlp(block_expert, slot_tok.reshape(n_blocks, 1, MOE_BLK), h1_rows,
                          jnp.broadcast_to(slot_gate[:, None], (n_blocks * MOE_BLK, LANES)),
                          w_gate_up[0].astype(BF16), b_gate_up[0][:, None, :],
                          w_down[0].astype(BF16), b_down[0][:, None, :])
    pos3 = pos_slots.reshape(npt + 1, ROW_TILE, TOP_K).transpose(0, 2, 1).reshape(npt + 1, 1, TOP_K * ROW_TILE)
    y_prompt, y_sample = _combine(pos3, ys_rows, h1_rows, ln2_w, ln2_b, npt)

    shift_p = za[s - 1:n_p:s, :RWKV_COLS].reshape(1, b, 1, RWKV_COLS)
    shift_s = za[n_p + t - 1:n_p + db * t:t, :RWKV_COLS].reshape(1, db, 1, RWKV_COLS)
    heads = lambda u: u.reshape(u.shape[0], u.shape[1], KV_HEADS, HEAD_DIM)
    meta_k_p = jnp.broadcast_to(heads(km_p[:, :N_META]), (b, N_META, KV_HEADS, HEAD_DIM))[None]
    meta_v_p = jnp.broadcast_to(heads(vm_p[:, :N_META]), (b, N_META, KV_HEADS, HEAD_DIM))[None]
    win_k_p = heads(k_all[:n_p].reshape(b, s, kvd)[:, -WINDOW:])[None]
    win_v_p = heads(v_all[:n_p].reshape(b, s, kvd)[:, -WINDOW:])[None]
    win_k_s = heads(keys_s[:, -WINDOW:])[None]
    win_v_s = heads(vals_s[:, -WINDOW:])[None]
    return (y_prompt.reshape(b, s, D_MODEL), y_sample.reshape(db, t, D_MODEL),
            shift_p, wkv_p[None], meta_k_p, meta_v_p, win_k_p, win_v_p,
            shift_s, wkv_s[None], win_k_s, win_v_s)
```
